```python
import jax, jax.numpy as jnp
from jax import lax
import numpy as np

D_MODEL = 1024
BATCH = 16
SEQ = 256
DEPTH = 4
DEC_BATCH = 8
DEC_SEQ = 4096
PAST_LEN = 256

GRID_W = 64
N_MIXERS = 3
N_GLA_LAYERS = (DEPTH + 2) // 3
N_MLSTM_LAYERS = (DEPTH + 1) // 3
N_MLA_LAYERS = DEPTH // 3
N_DENSE_LAYERS = (DEPTH + 1) // 2
N_MOE_LAYERS = DEPTH // 2
EPS = 1e-6
CHUNK = 64
GLA_HEADS = 4
GLA_DK = D_MODEL // (2 * GLA_HEADS)
GLA_DV = D_MODEL // GLA_HEADS
GLA_GATE_RANK = 16
GLA_GATE_NORM = 16.0
MLSTM_HEADS = 4
MLSTM_DK = D_MODEL // (2 * MLSTM_HEADS)
MLSTM_DV = D_MODEL // MLSTM_HEADS
MLA_HEADS = 16
MLA_Q_LORA = D_MODEL // 4
MLA_KV_LORA = D_MODEL // 8
MLA_NOPE = 128
MLA_ROPE = 64
MLA_V = 128
ROPE_AXIS = MLA_ROPE // 2
ROPE_THETA = 10000.0
Q_BLOCK = 128
D_FF = 2816
N_EXPERTS = 8
TOP_K = 2
D_FF_EXPERT = 3584
MOE_BLOCK = 256

kernel_name = 'hybrid_gla_mlstm_mla_diffusion_step'


def rmsnorm(x, g):
    xf = x.astype(jnp.float32)
    y = xf * lax.rsqrt(jnp.mean(xf * xf, axis=-1, keepdims=True) + EPS)
    return (y * g.astype(jnp.float32)).astype(x.dtype)


def adaln(cond, w, b):
    return jnp.split(jax.nn.silu(cond) @ w + b, 6, axis=-1)


def flip_t(a):
    return jnp.flip(a, axis=2)


def to_chunks(a):
    b, h, l = a.shape[:3]
    a = a.reshape((b, h, l // CHUNK, CHUNK) + a.shape[3:])
    return jnp.moveaxis(a, 2, 0)


def from_chunks(a):
    n, b, h, c, d = a.shape
    return jnp.moveaxis(a, 0, 2).reshape(b, h, n * c, d)


def gla_scan(q, k, v, lg, s0):
    mask = jnp.tril(jnp.ones((CHUNK, CHUNK), dtype=bool))

    def step(s, inp):
        qc, kc, vc, gc = inp
        b = jnp.cumsum(gc, axis=-2)
        b_last = b[..., -1:, :]
        q_in = qc * jnp.exp(b)
        k_in = kc * jnp.exp(-b)
        a = jnp.where(mask, jnp.einsum('bhid,bhjd->bhij', q_in, k_in), 0.0)
        o = jnp.einsum('bhid,bhde->bhie', q_in, s) + jnp.einsum('bhij,bhje->bhie', a, vc)
        s = jnp.exp(b_last)[..., 0, :, None] * s + jnp.einsum('bhjd,bhje->bhde', kc * jnp.exp(b_last - b), vc)
        return s, o

    s, o = lax.scan(step, s0, (to_chunks(q), to_chunks(k), to_chunks(v), to_chunks(lg)))
    return from_chunks(o), s


def gla_mixer(h, s0, w_in, w_gk1, w_gk2, b_gk, g_out, w_out):
    bsz, seqlen, _ = h.shape
    hk, hv = GLA_HEADS * GLA_DK, GLA_HEADS * GLA_DV
    q, k, v, g = jnp.split(h @ w_in, [hk, 2 * hk, 2 * hk + hv], axis=-1)

    def heads(a, d):
        return a.reshape(bsz, seqlen, GLA_HEADS, d).transpose(0, 2, 1, 3).astype(jnp.float32)

    q = heads(q, GLA_DK) * (GLA_DK ** -0.5)
    k = heads(k, GLA_DK)
    v = heads(v, GLA_DV)
    z = jnp.einsum('bld,edr->eblr', h, w_gk1)
    lg = jnp.einsum('eblr,erk->eblk', z, w_gk2) + b_gk[:, None, None, :]
    lg = jax.nn.log_sigmoid(lg.astype(jnp.float32)) / GLA_GATE_NORM
    lg = lg.reshape(2, bsz, seqlen, GLA_HEADS, GLA_DK).transpose(0, 1, 3, 2, 4)
    s0 = s0.astype(jnp.float32)
    o_f, s_f = gla_scan(q, k, v, lg[0], s0[:, 0])
    o_b, s_b = gla_scan(flip_t(q), flip_t(k), flip_t(v), flip_t(lg[1]), s0[:, 1])
    o = (o_f + flip_t(o_b)).transpose(0, 2, 1, 3)
    o = rmsnorm(o, g_out).reshape(bsz, seqlen, hv).astype(h.dtype)
    out = (o * jax.nn.silu(g)) @ w_out
    return out, jnp.stack([s_f, s_b], axis=1)


def mlstm_scan(q, k, v, ig, lf, c0, n0, m0):
    mask = jnp.tril(jnp.ones((CHUNK, CHUNK), dtype=bool))

    def step(carry, inp):
        cs, ns, ms = carry
        qc, kc, vc, ic, fc = inp
        f_cum = jnp.cumsum(fc, axis=-1)
        dmat = f_cum[..., :, None] - f_cum[..., None, :] + ic[..., None, :]
        dmat = jnp.where(mask, dmat, -jnp.inf)
        inter = f_cum + ms[..., None]
        m = jnp.maximum(inter, jnp.max(dmat, axis=-1))
        w = jnp.exp(dmat - m[..., None])
        s_inter = jnp.exp(inter - m)
        qk = jnp.einsum('bhid,bhjd->bhij', qc, kc) * w
        num = s_inter[..., None] * jnp.einsum('bhid,bhde->bhie', qc, cs) + jnp.einsum('bhij,bhje->bhie', qk, vc)
        den = s_inter * jnp.einsum('bhid,bhd->bhi', qc, ns) + jnp.sum(qk, axis=-1)
        hout = num / jnp.maximum(jnp.abs(den), jnp.exp(-m))[..., None]
        g_last = f_cum[..., -1:] - f_cum + ic
        m_new = jnp.maximum(f_cum[..., -1] + ms, jnp.max(g_last, axis=-1))
        wk = jnp.exp(g_last - m_new[..., None])
        s_dec = jnp.exp(f_cum[..., -1] + ms - m_new)
        c_new = s_dec[..., None, None] * cs + jnp.einsum('bhjd,bhje->bhde', kc * wk[..., None], vc)
        n_new = s_dec[..., None] * ns + jnp.einsum('bhj,bhjd->bhd', wk, kc)
        return (c_new, n_new, m_new), hout

    (cs, ns, ms), hs = lax.scan(step, (c0, n0, m0),
                                (to_chunks(q), to_chunks(k), to_chunks(v), to_chunks(ig), to_chunks(lf)))
    return from_chunks(hs), cs, ns, ms


def mlstm_mixer(h, c0, n0, m0, w_in, w_gates, b_gates, g_out, w_out):
    bsz, seqlen, _ = h.shape
    hk, hv = MLSTM_HEADS * MLSTM_DK, MLSTM_HEADS * MLSTM_DV
    q, k, v, o = jnp.split(h @ w_in, [hk, 2 * hk, 2 * hk + hv], axis=-1)

    def heads(a, d):
        return a.reshape(bsz, seqlen, MLSTM_HEADS, d).transpose(0, 2, 1, 3).astype(jnp.float32)

    q = heads(q, MLSTM_DK) * (MLSTM_DK ** -0.5)
    k = heads(k, MLSTM_DK)
    v = heads(v, MLSTM_DV)
    gates = (jnp.einsum('bld,edg->ebgl', h, w_gates) + b_gates[:, None, :, None]).astype(jnp.float32)
    ig = gates[:, :, :MLSTM_HEADS]
    lf = jax.nn.log_sigmoid(gates[:, :, MLSTM_HEADS:])
    c0, n0, m0 = c0.astype(jnp.float32), n0.astype(jnp.float32), m0.astype(jnp.float32)
    h_f, c_f, n_f, m_f = mlstm_scan(q, k, v, ig[0], lf[0], c0[:, 0], n0[:, 0], m0[:, 0])
    h_b, c_b, n_b, m_b = mlstm_scan(flip_t(q), flip_t(k), flip_t(v), flip_t(ig[1]), flip_t(lf[1]),
                                    c0[:, 1], n0[:, 1], m0[:, 1])
    hs = (h_f + flip_t(h_b)).transpose(0, 2, 1, 3)
    hs = rmsnorm(hs, g_out.reshape(MLSTM_HEADS, MLSTM_DV)).reshape(bsz, seqlen, hv).astype(h.dtype)
    out = (hs * jax.nn.sigmoid(o)) @ w_out
    return out, jnp.stack([c_f, c_b], axis=1), jnp.stack([n_f, n_b], axis=1), jnp.stack([m_f, m_b], axis=1)


def rope_tables(rows):
    r = jnp.repeat(jnp.arange(rows, dtype=jnp.float32), GRID_W)
    col = jnp.tile(jnp.arange(GRID_W, dtype=jnp.float32), rows)
    inv = ROPE_THETA ** (-jnp.arange(0, ROPE_AXIS, 2, dtype=jnp.float32) / ROPE_AXIS)
    ang_r = r[:, None] * inv
    ang_c = col[:, None] * inv
    return (jnp.cos(ang_r), jnp.sin(ang_r), jnp.cos(ang_c), jnp.sin(ang_c))


def rotate(x, cos, sin):
    x1, x2 = jnp.split(x.astype(jnp.float32), 2, axis=-1)
    return jnp.concatenate([x1 * cos - x2 * sin, x1 * sin + x2 * cos], axis=-1)


def axial_rope(x, tables, head_axis):
    cr, sr, cc, sc = tables
    if head_axis:
        cr, sr, cc, sc = cr[:, None, :], sr[:, None, :], cc[:, None, :], sc[:, None, :]
    out = jnp.concatenate([rotate(x[..., :ROPE_AXIS], cr, sr), rotate(x[..., ROPE_AXIS:], cc, sc)], axis=-1)
    return out.astype(x.dtype)


def mla_queries(h, w_dq, g_q, w_uq, tables):
    bsz, seqlen, _ = h.shape
    q = (rmsnorm(h @ w_dq, g_q) @ w_uq).reshape(bsz, seqlen, MLA_HEADS, MLA_NOPE + MLA_ROPE)
    if tables is not None:
        q = jnp.concatenate([q[..., :MLA_NOPE], axial_rope(q[..., MLA_NOPE:], tables, True)], axis=-1)
    return q.transpose(0, 2, 1, 3)


def mla_kv_latent(h, w_dkv, g_kv, tables):
    kv = h @ w_dkv
    c_kv = rmsnorm(kv[..., :MLA_KV_LORA], g_kv)
    k_pe = kv[..., MLA_KV_LORA:]
    if tables is not None:
        k_pe = axial_rope(k_pe, tables, False)
    return c_kv, k_pe


def mla_expand(c_kv, k_pe, w_ukv):
    bsz, seqlen, _ = c_kv.shape
    kv = (c_kv @ w_ukv).reshape(bsz, seqlen, MLA_HEADS, MLA_NOPE + MLA_V)
    k = jnp.concatenate([kv[..., :MLA_NOPE],
                         jnp.broadcast_to(k_pe[:, :, None, :], (bsz, seqlen, MLA_HEADS, MLA_ROPE))], axis=-1)
    return k.transpose(0, 2, 1, 3), kv[..., MLA_NOPE:].transpose(0, 2, 1, 3)


def block_attention(q, k, v):
    bsz, nh, lq, dq = q.shape
    scale = dq ** -0.5
    qb = jnp.moveaxis(q.reshape(bsz, nh, lq // Q_BLOCK, Q_BLOCK, dq), 2, 0)

    def attend(q_blk):
        s = jnp.einsum('bhqd,bhkd->bhqk', q_blk, k).astype(jnp.float32) * scale
        p = jax.nn.softmax(s, axis=-1).astype(v.dtype)
        return jnp.einsum('bhqk,bhkd->bhqd', p, v)

    o = lax.map(attend, qb)
    return jnp.moveaxis(o, 0, 2).reshape(bsz, nh, lq, v.shape[-1])


def mla_output(o, w_out):
    bsz, nh, seqlen, dv = o.shape
    return o.transpose(0, 2, 1, 3).reshape(bsz, seqlen, nh * dv) @ w_out


def mla_context(h, w_dq, g_q, w_uq, w_dkv, g_kv, w_ukv, w_out):
    q = mla_queries(h, w_dq, g_q, w_uq, None)
    c_kv, k_pe = mla_kv_latent(h, w_dkv, g_kv, None)
    k, v = mla_expand(c_kv, k_pe, w_ukv)
    return mla_output(block_attention(q, k, v), w_out), c_kv, k_pe


def mla_latent(h, ctx_ckv, ctx_kpe, tables, w_dq, g_q, w_uq, w_dkv, g_kv, w_ukv, w_out):
    q = mla_queries(h, w_dq, g_q, w_uq, tables)
    c_lat, kpe_lat = mla_kv_latent(h, w_dkv, g_kv, tables)
    c_all = jnp.concatenate([c_lat, ctx_ckv.astype(c_lat.dtype)], axis=1)
    kpe_all = jnp.concatenate([kpe_lat, ctx_kpe.astype(kpe_lat.dtype)], axis=1)
    k, v = mla_expand(c_all, kpe_all, w_ukv)
    return mla_output(block_attention(q, k, v), w_out)


def swiglu(h, w_gate, w_up, w_down):
    return (jax.nn.silu(h @ w_gate) * (h @ w_up)) @ w_down


def moe_swiglu(h, w_router, w_gate, w_up, w_down):
    bsz, seqlen, d = h.shape
    x = h.reshape(-1, d)
    n_tok = x.shape[0]
    logits = (x @ w_router).astype(jnp.float32)
    top_logit, top_e = lax.top_k(logits, TOP_K)
    top_w = jax.nn.softmax(top_logit, axis=-1)
    n_assign = n_tok * TOP_K
    flat_e = top_e.reshape(-1).astype(jnp.int32)
    flat_w = top_w.reshape(-1)
    flat_tok = jnp.repeat(jnp.arange(n_tok, dtype=jnp.int32), TOP_K)
    onehot = jax.nn.one_hot(flat_e, N_EXPERTS, dtype=jnp.int32)
    rank = jnp.take_along_axis(jnp.cumsum(onehot, axis=0), flat_e[:, None], axis=1)[:, 0] - 1
    counts = jnp.sum(onehot, axis=0)
    padded = (counts + MOE_BLOCK - 1) // MOE_BLOCK * MOE_BLOCK
    pad_end = jnp.cumsum(padded)
    pad_start = pad_end - padded
    dest = pad_start[flat_e] + rank
    n_slots = ((n_assign + MOE_BLOCK - 1) // MOE_BLOCK + N_EXPERTS) * MOE_BLOCK
    n_blk = n_slots // MOE_BLOCK
    slot_tok = jnp.zeros((n_slots,), jnp.int32).at[dest].set(flat_tok)
    slot_w = jnp.zeros((n_slots,), jnp.float32).at[dest].set(flat_w)
    blk_start = jnp.arange(n_blk, dtype=jnp.int32) * MOE_BLOCK
    blk_e = jnp.minimum(jnp.searchsorted(pad_end, blk_start, side='right'), N_EXPERTS - 1)
    xb = x[slot_tok].reshape(n_blk, MOE_BLOCK, d)

    def expert_block(args):
        xblk, e = args
        return (jax.nn.silu(xblk @ w_gate[e]) * (xblk @ w_up[e])) @ w_down[e]

    yb = lax.map(expert_block, (xb, blk_e)).reshape(n_slots, d)
    y = jnp.zeros_like(x).at[slot_tok].add(yb * slot_w[:, None].astype(yb.dtype))
    return y.reshape(bsz, seqlen, d)


def setup_inputs(seed: int = 0) -> dict:
    key = jax.random.key(seed)
    ks = iter(jax.random.split(key, 64))

    def nrm(shape, scale):
        return jax.random.normal(next(ks), shape, jnp.float32) * scale

    def gain(shape):
        return 1.0 + 0.02 * jax.random.normal(next(ks), shape, jnp.float32)

    d = D_MODEL
    gla_hk, gla_hv = GLA_HEADS * GLA_DK, GLA_HEADS * GLA_DV
    ml_hk, ml_hv = MLSTM_HEADS * MLSTM_DK, MLSTM_HEADS * MLSTM_DV
    mla_qd = MLA_HEADS * (MLA_NOPE + MLA_ROPE)
    mla_kvd = MLA_HEADS * (MLA_NOPE + MLA_V)
    mla_od = MLA_HEADS * MLA_V
    forget_bias = jnp.linspace(3.0, 6.0, MLSTM_HEADS, dtype=jnp.float32)
    mlstm_b_gates = jnp.concatenate([nrm((N_MLSTM_LAYERS, 2, MLSTM_HEADS), 0.1),
                                     forget_bias + nrm((N_MLSTM_LAYERS, 2, MLSTM_HEADS), 0.1)], axis=-1)
    return {
        'x_prompt': nrm((BATCH, SEQ, d), 1.0),
        'x_sample': nrm((DEC_BATCH, DEC_SEQ, d), 1.0),
        'state_gla': nrm((DEC_BATCH, N_GLA_LAYERS, 2, GLA_HEADS, GLA_DK, GLA_DV), 1.0),
        'state_mlstm_c': nrm((DEC_BATCH, N_MLSTM_LAYERS, 2, MLSTM_HEADS, MLSTM_DK, MLSTM_DV), 1.0),
        'state_mlstm_n': nrm((DEC_BATCH, N_MLSTM_LAYERS, 2, MLSTM_HEADS, MLSTM_DK), 1.0),
        'state_mlstm_m': 1.0 + nrm((DEC_BATCH, N_MLSTM_LAYERS, 2, MLSTM_HEADS), 0.5),
        'cache_mla_ckv': nrm((DEC_BATCH, N_MLA_LAYERS, PAST_LEN, MLA_KV_LORA), 1.0),
        'cache_mla_kpe': nrm((DEC_BATCH, N_MLA_LAYERS, PAST_LEN, MLA_ROPE), 1.0),
        'c': nrm((DEC_BATCH, d), 1.0),
        'c_ctx': nrm((d,), 1.0),
        'ada_w': nrm((DEPTH, d, 6 * d), 0.5 * d ** -0.5),
        'ada_b': nrm((DEPTH, 6 * d), 0.02),
        'norm1_g': gain((DEPTH, d)),
        'norm2_g': gain((DEPTH, d)),
        'final_g': gain((d,)),
        'gla_w_in': nrm((N_GLA_LAYERS, d, 2 * gla_hk + 2 * gla_hv), d ** -0.5),
        'gla_w_gk1': nrm((N_GLA_LAYERS, 2, d, GLA_GATE_RANK), d ** -0.5),
        'gla_w_gk2': nrm((N_GLA_LAYERS, 2, GLA_GATE_RANK, gla_hk), GLA_GATE_RANK ** -0.5),
        'gla_b_gk': nrm((N_GLA_LAYERS, 2, gla_hk), 0.1),
        'gla_g_out': gain((N_GLA_LAYERS, GLA_DV)),
        'gla_w_out': nrm((N_GLA_LAYERS, gla_hv, d), gla_hv ** -0.5),
        'mlstm_w_in': nrm((N_MLSTM_LAYERS, d, 2 * ml_hk + 2 * ml_hv), d ** -0.5),
        'mlstm_w_gates': nrm((N_MLSTM_LAYERS, 2, d, 2 * MLSTM_HEADS), d ** -0.5),
        'mlstm_b_gates': mlstm_b_gates,
        'mlstm_g_out': gain((N_MLSTM_LAYERS, ml_hv)),
        'mlstm_w_out': nrm((N_MLSTM_LAYERS, ml_hv, d), ml_hv ** -0.5),
        'mla_w_dq': nrm((N_MLA_LAYERS, d, MLA_Q_LORA), d ** -0.5),
        'mla_g_q': gain((N_MLA_LAYERS, MLA_Q_LORA)),
        'mla_w_uq': nrm((N_MLA_LAYERS, MLA_Q_LORA, mla_qd), MLA_Q_LORA ** -0.5),
        'mla_w_dkv': nrm((N_MLA_LAYERS, d, MLA_KV_LORA + MLA_ROPE), d ** -0.5),
        'mla_g_kv': gain((N_MLA_LAYERS, MLA_KV_LORA)),
        'mla_w_ukv': nrm((N_MLA_LAYERS, MLA_KV_LORA, mla_kvd), MLA_KV_LORA ** -0.5),
        'mla_w_out': nrm((N_MLA_LAYERS, mla_od, d), mla_od ** -0.5),
        'ffn_w_gate': nrm((N_DENSE_LAYERS, d, D_FF), d ** -0.5),
        'ffn_w_up': nrm((N_DENSE_LAYERS, d, D_FF), d ** -0.5),
        'ffn_w_down': nrm((N_DENSE_LAYERS, D_FF, d), D_FF ** -0.5),
        'moe_w_router': nrm((N_MOE_LAYERS, d, N_EXPERTS), d ** -0.5),
        'moe_w_gate': nrm((N_MOE_LAYERS, N_EXPERTS, d, D_FF_EXPERT), d ** -0.5),
        'moe_w_up': nrm((N_MOE_LAYERS, N_EXPERTS, d, D_FF_EXPERT), d ** -0.5),
        'moe_w_down': nrm((N_MOE_LAYERS, N_EXPERTS, D_FF_EXPERT, d), D_FF_EXPERT ** -0.5),
    }


def reference(x_prompt, x_sample, state_gla, state_mlstm_c, state_mlstm_n, state_mlstm_m,
              cache_mla_ckv, cache_mla_kpe, c, c_ctx, ada_w, ada_b, norm1_g, norm2_g, final_g,
              gla_w_in, gla_w_gk1, gla_w_gk2, gla_b_gk, gla_g_out, gla_w_out,
              mlstm_w_in, mlstm_w_gates, mlstm_b_gates, mlstm_g_out, mlstm_w_out,
              mla_w_dq, mla_g_q, mla_w_uq, mla_w_dkv, mla_g_kv, mla_w_ukv, mla_w_out,
              ffn_w_gate, ffn_w_up, ffn_w_down,
              moe_w_router, moe_w_gate, moe_w_up, moe_w_down):
    bp = x_prompt.shape[0]
    rows = x_sample.shape[1] // GRID_W
    lat_tables = rope_tables(rows)
    cond_p = c_ctx[None, None, :]
    cond_s = c[:, None, :]
    xp, xs = x_prompt, x_sample
    new_gla, new_mc, new_mn, new_mm, new_ckv, new_kpe = [], [], [], [], [], []
    for i in range(DEPTH):
        sh1p, sc1p, gt1p, sh2p, sc2p, gt2p = adaln(cond_p, ada_w[i], ada_b[i])
        sh1s, sc1s, gt1s, sh2s, sc2s, gt2s = adaln(cond_s, ada_w[i], ada_b[i])
        hp = rmsnorm(xp, norm1_g[i]) * (1.0 + sc1p) + sh1p
        hs = rmsnorm(xs, norm1_g[i]) * (1.0 + sc1s) + sh1s
        j = i // N_MIXERS
        if i % N_MIXERS == 0:
            wts = (gla_w_in[j], gla_w_gk1[j], gla_w_gk2[j], gla_b_gk[j], gla_g_out[j], gla_w_out[j])
            s0 = jnp.zeros((bp, 2, GLA_HEADS, GLA_DK, GLA_DV), jnp.float32)
            mp, st = gla_mixer(hp, s0, *wts)
            ms, _ = gla_mixer(hs, state_gla[:, j], *wts)
            new_gla.append(st)
        elif i % N_MIXERS == 1:
            wts = (mlstm_w_in[j], mlstm_w_gates[j], mlstm_b_gates[j], mlstm_g_out[j], mlstm_w_out[j])
            c0 = jnp.zeros((bp, 2, MLSTM_HEADS, MLSTM_DK, MLSTM_DV), jnp.float32)
            n0 = jnp.zeros((bp, 2, MLSTM_HEADS, MLSTM_DK), jnp.float32)
            m0 = jnp.zeros((bp, 2, MLSTM_HEADS), jnp.float32)
            mp, sc_, sn_, sm_ = mlstm_mixer(hp, c0, n0, m0, *wts)
            ms, _, _, _ = mlstm_mixer(hs, state_mlstm_c[:, j], state_mlstm_n[:, j], state_mlstm_m[:, j], *wts)
            new_mc.append(sc_)
            new_mn.append(sn_)
            new_mm.append(sm_)
        else:
            wts = (mla_w_dq[j], mla_g_q[j], mla_w_uq[j], mla_w_dkv[j], mla_g_kv[j], mla_w_ukv[j], mla_w_out[j])
            mp, ckv, kpe = mla_context(hp, *wts)
            ms = mla_latent(hs, cache_mla_ckv[:, j], cache_mla_kpe[:, j], lat_tables, *wts)
            new_ckv.append(ckv)
            new_kpe.append(kpe)
        xp = xp + gt1p * mp
        xs = xs + gt1s * ms
        hp = rmsnorm(xp, norm2_g[i]) * (1.0 + sc2p) + sh2p
        hs = rmsnorm(xs, norm2_g[i]) * (1.0 + sc2s) + sh2s
        f = i // 2
        if i % 2 == 0:
            fp = swiglu(hp, ffn_w_gate[f], ffn_w_up[f], ffn_w_down[f])
            fs = swiglu(hs, ffn_w_gate[f], ffn_w_up[f], ffn_w_down[f])
        else:
            fp = moe_swiglu(hp, moe_w_router[f], moe_w_gate[f], moe_w_up[f], moe_w_down[f])
            fs = moe_swiglu(hs, moe_w_router[f], moe_w_gate[f], moe_w_up[f], moe_w_down[f])
        xp = xp + gt2p * fp
        xs = xs + gt2s * fs
    y_prompt = rmsnorm(xp, final_g)
    y_sample = rmsnorm(xs, final_g)
    new_state_gla = jnp.stack(new_gla, axis=1)
    new_state_mlstm_c = jnp.stack(new_mc, axis=1)
    new_state_mlstm_n = jnp.stack(new_mn, axis=1)
    new_state_mlstm_m = jnp.stack(new_mm, axis=1)
    new_cache_mla_ckv = jnp.stack(new_ckv, axis=1)
    new_cache_mla_kpe = jnp.stack(new_kpe, axis=1)
    return (y_prompt, y_sample, new_state_gla, new_state_mlstm_c, new_state_mlstm_n, new_state_mlstm_m, new_cache_mla_ckv, new_cache_mla_kpe)
```

```python
import functools

import jax
import jax.numpy as jnp
from jax import lax
from jax.experimental import pallas as pl
from jax.experimental.pallas import tpu as pltpu

F32 = jnp.float32
BF16 = jnp.bfloat16

EPS = 1e-6
CHUNK = 64
LIN_HEADS = 4
LIN_DK = 128
LIN_DV = 256
GLA_GATE_NORM = 16.0
MLA_HEADS = 16
MLA_Q_LORA = 256
MLA_KV_LORA = 128
MLA_NOPE = 128
MLA_ROPE = 64
MLA_V = 128
MLA_QK_PAD = 256
GRID_W = 64
ROPE_THETA = 10000.0
N_EXPERTS = 8
MOE_TILE = 512
LANES = 128
N_MODS_PAD = 16

VMEM_LIMIT = 56 * 1024 * 1024


def _cparams(*sem):
    return pltpu.CompilerParams(dimension_semantics=sem, vmem_limit_bytes=VMEM_LIMIT)


def _sigmoid(x):
    return 1.0 / (1.0 + jnp.exp(-x))


def _log_sigmoid(x):
    return jnp.minimum(x, 0.0) - jnp.log1p(jnp.exp(-jnp.abs(x)))


def _split_bf16(a):
    hi = a.astype(BF16)
    lo = (a - hi.astype(F32)).astype(BF16)
    return hi, lo


def _dot(a, b):
    return jnp.dot(a, b, preferred_element_type=F32)


def _dot_nt(a, b):
    return lax.dot_general(a, b, (((1,), (1,)), ((), ())), preferred_element_type=F32)


def _dot_tn(a, b):
    return lax.dot_general(a, b, (((0,), (0,)), ((), ())), preferred_element_type=F32)


def _dot3(a, b):
    ah, al = _split_bf16(a)
    bh, bl = _split_bf16(b)
    return _dot(ah, bh) + _dot(ah, bl) + _dot(al, bh)


def _normmod(x, g, sc, sh):
    ms = jnp.mean(x * x, axis=-1, keepdims=True)
    return (x * lax.rsqrt(ms + EPS) * g) * (1.0 + sc) + sh


def _head_rmsnorm(o, width):
    parts = []
    for h in range(o.shape[1] // width):
        seg = o[:, h * width:(h + 1) * width]
        ms = jnp.mean(seg * seg, axis=-1, keepdims=True)
        parts.append(seg * lax.rsqrt(ms + EPS))
    return jnp.concatenate(parts, axis=1)


def _ada_kernel(c_ref, w_ref, b_ref, o_ref):
    c = c_ref[...]
    o_ref[...] = _dot3(c * _sigmoid(c), w_ref[...]) + b_ref[...]


def _ada_mods(cond, ada_w, ada_b):
    depth, d, n = ada_w.shape
    tn = 1536
    out = pl.pallas_call(
        _ada_kernel,
        out_shape=jax.ShapeDtypeStruct((depth, N_MODS_PAD, n), F32),
        grid=(depth, n // tn),
        in_specs=[
            pl.BlockSpec((N_MODS_PAD, d), lambda l, j: (0, 0)),
            pl.BlockSpec((None, d, tn), lambda l, j: (l, 0, j)),
            pl.BlockSpec((None, 1, tn), lambda l, j: (l, 0, j)),
        ],
        out_specs=pl.BlockSpec((None, N_MODS_PAD, tn), lambda l, j: (l, 0, j)),
        compiler_params=_cparams("parallel", "parallel"),
        name="ada_mods",
    )(cond, ada_w, ada_b.reshape(depth, 1, n))
    return out.reshape(depth, N_MODS_PAD, 1, n)


def _mod_spec(layer, chunk, tm, group, d):
    return pl.BlockSpec((None, None, 1, d), lambda i, *_: (layer, (i * tm) // group, 0, chunk))


def _gla_in_kernel(x_ref, g_ref, sc_ref, sh_ref, w_ref, w1_ref, w2_ref, b_ref, o_ref, lg_ref):
    h = _normmod(x_ref[...], g_ref[...], sc_ref[...], sh_ref[...]).astype(BF16)
    o_ref[...] = _dot(h, w_ref[...])
    z = _dot(h, w1_ref[...]).astype(BF16)
    lg_ref[...] = _log_sigmoid(_dot(z, w2_ref[...]) + b_ref[...]) * (1.0 / GLA_GATE_NORM)


def _mlstm_in_kernel(x_ref, g_ref, sc_ref, sh_ref, w_ref, wg_ref, b_ref, o_ref, gt_ref):
    hf = _normmod(x_ref[...], g_ref[...], sc_ref[...], sh_ref[...])
    o_ref[...] = _dot(hf.astype(BF16), w_ref[...])
    gates = _dot3(hf, wg_ref[...]) + b_ref[...]
    lane = lax.broadcasted_iota(jnp.int32, gates.shape, 1)
    is_forget = (lane & LIN_HEADS) != 0
    gt_ref[...] = jnp.where(is_forget, _log_sigmoid(gates), gates)


def _mixer_in(kernel, x, norm_g, mods, layer, group, w_in, extra, extra_out_w, tm=512):
    t, d = x.shape
    n = w_in.shape[1]
    tm = min(tm, group)
    full = lambda a: pl.BlockSpec(a.shape, lambda i: (0,) * a.ndim)
    return pl.pallas_call(
        kernel,
        out_shape=(jax.ShapeDtypeStruct((t, n), F32), jax.ShapeDtypeStruct((t, extra_out_w), F32)),
        grid=(t // tm,),
        in_specs=[
            pl.BlockSpec((tm, d), lambda i: (i, 0)),
            pl.BlockSpec((1, d), lambda i: (0, 0)),
            _mod_spec(layer, 1, tm, group, d),
            _mod_spec(layer, 0, tm, group, d),
            full(w_in),
        ] + [full(a) for a in extra],
        out_specs=(pl.BlockSpec((tm, n), lambda i: (i, 0)),
                   pl.BlockSpec((tm, extra_out_w), lambda i: (i, 0))),
        compiler_params=_cparams("parallel"),
        name=kernel.__name__.strip("_"),
    )(x, norm_g.reshape(1, d), mods, mods, w_in, *extra)


def _chunk_masks():
    row = lax.broadcasted_iota(jnp.int32, (CHUNK, CHUNK), 0)
    col = lax.broadcasted_iota(jnp.int32, (CHUNK, CHUNK), 1)
    return (col <= row, col >= row), row == col


def _gla_scan_kernel(qf, kf, vf, gf, qb, kb, vb, gb, s0_ref, of_ref, ob_ref, sfin_ref, st_ref):
    c = pl.program_id(1)
    nc = pl.num_programs(1)

    @pl.when(c == 0)
    def _():
        for d in range(2):
            for h in range(LIN_HEADS):
                st_ref[d, h] = s0_ref[d, h].T

    incls, _ = _chunk_masks()
    scale = LIN_DK ** -0.5
    hk = LIN_HEADS * LIN_DK
    dirs = ((qf, kf, vf, gf, of_ref), (qb, kb, vb, gb, ob_ref))
    for d, (q_ref, k_ref, v_ref, g_ref, o_ref) in enumerate(dirs):
        incl = incls[d]
        tri = jnp.where(incl, 1.0, 0.0).astype(BF16)
        g = g_ref[...]
        g1 = g.astype(BF16)
        r1 = g - g1.astype(F32)
        g2 = r1.astype(BF16)
        g3 = (r1 - g2.astype(F32)).astype(BF16)
        bc = _dot(tri, jnp.concatenate([g1, g2, g3], axis=1))
        b_all = bc[:, :hk] + bc[:, hk:2 * hk] + bc[:, 2 * hk:]
        last = CHUNK - 1 if d == 0 else 0
        for h in range(LIN_HEADS):
            ks = slice(h * LIN_DK, (h + 1) * LIN_DK)
            vs = slice(h * LIN_DV, (h + 1) * LIN_DV)
            b = b_all[:, ks]
            btot = b[last:last + 1, :]
            q = q_ref[:, ks] * scale
            k = k_ref[:, ks]
            v = v_ref[:, vs].astype(BF16)
            q_in = (q * jnp.exp(b)).astype(BF16)
            k_in = (k * jnp.exp(-b)).astype(BF16)
            a = jnp.where(incl, _dot_nt(q_in, k_in), 0.0).astype(BF16)
            st = st_ref[d, h]
            o_ref[:, vs] = _dot_nt(q_in, st.astype(BF16)) + _dot(a, v)
            k_dec = (k * jnp.exp(btot - b)).astype(BF16)
            st_ref[d, h] = st * jnp.exp(btot) + _dot_tn(v, k_dec)

    @pl.when(c == nc - 1)
    def _():
        for d in range(2):
            for h in range(LIN_HEADS):
                sfin_ref[d, h] = st_ref[d, h].T


def _seq_specs(width, col_block, row0, nc):
    fwd = pl.BlockSpec((CHUNK, width), lambda b, c: (row0 + b * nc + c, col_block))
    bwd = pl.BlockSpec((CHUNK, width), lambda b, c: (row0 + b * nc + (nc - 1 - c), col_block))
    return fwd, bwd


def _gla_scan(qkvg, lg, s0, row0, nseq, seqlen):
    nc = seqlen // CHUNK
    r0 = row0 // CHUNK
    hk, hv = LIN_HEADS * LIN_DK, LIN_HEADS * LIN_DV
    qf, qb = _seq_specs(hk, 0, r0, nc)
    kf, kb = _seq_specs(hk, 1, r0, nc)
    vf, vb = _seq_specs(hv, 1, r0, nc)
    gf, gb = _seq_specs(hk, 0, r0, nc)[0], _seq_specs(hk, 1, r0, nc)[1]
    st_shape = (2, LIN_HEADS, LIN_DK, LIN_DV)
    st_spec = pl.BlockSpec((None,) + st_shape, lambda b, c: (b, 0, 0, 0, 0))
    o_f = pl.BlockSpec((CHUNK, hv), lambda b, c: (b * nc + c, 0))
    o_b = pl.BlockSpec((CHUNK, hv), lambda b, c: (b * nc + (nc - 1 - c), 0))
    n = nseq * seqlen
    return pl.pallas_call(
        _gla_scan_kernel,
        out_shape=(jax.ShapeDtypeStruct((n, hv), F32), jax.ShapeDtypeStruct((n, hv), F32),
                   jax.ShapeDtypeStruct((nseq,) + st_shape, F32)),
        grid=(nseq, nc),
        in_specs=[qf, kf, vf, gf, qb, kb, vb, gb, st_spec],
        out_specs=(o_f, o_b, st_spec),
        scratch_shapes=[pltpu.VMEM((2, LIN_HEADS, LIN_DV, LIN_DK), F32)],
        compiler_params=_cparams("parallel", "arbitrary"),
        name="gla_scan",
    )(qkvg, qkvg, qkvg, lg, qkvg, qkvg, qkvg, lg, s0)


def _mlstm_scan_kernel(qf, kf, vf, qb, kb, vb, gtf, gtb, c0_ref, n0_ref, m0_ref,
                       of_ref, ob_ref, cfin_ref, nfin_ref, mfin_ref, c_ref, n_ref, m_ref):
    c = pl.program_id(1)
    nc = pl.num_programs(1)

    @pl.when(c == 0)
    def _():
        for d in range(2):
            for h in range(LIN_HEADS):
                c_ref[d, h] = c0_ref[d, h].T
        n_ref[...] = n0_ref[...]
        m_ref[...] = m0_ref[...]

    incls, eye = _chunk_masks()
    scale = LIN_DK ** -0.5
    dirs = ((qf, kf, vf, gtf, of_ref), (qb, kb, vb, gtb, ob_ref))
    for d, (q_ref, k_ref, v_ref, gt_ref, o_ref) in enumerate(dirs):
        incl = incls[d]
        gt = gt_ref[...]
        for h in range(LIN_HEADS):
            ks = slice(h * LIN_DK, (h + 1) * LIN_DK)
            vs = slice(h * LIN_DV, (h + 1) * LIN_DV)
            r = d * LIN_HEADS + h
            lane = 2 * LIN_HEADS * d + h
            ig_col = gt[:, lane:lane + 1]
            lf_col = gt[:, lane + LIN_HEADS:lane + LIN_HEADS + 1]
            lf_row = jnp.sum(jnp.where(eye, lf_col, 0.0), axis=0, keepdims=True)
            fc_col = jnp.sum(jnp.where(incl, lf_row, 0.0), axis=1, keepdims=True)
            ftot = jnp.sum(lf_col, axis=0, keepdims=True)
            r_col = ig_col - fc_col
            r_row = jnp.sum(jnp.where(eye, r_col, 0.0), axis=0, keepdims=True)
            dmat = jnp.where(incl, fc_col + r_row, -jnp.inf)
            ms = m_ref[r:r + 1, 0:1]
            inter = fc_col + ms
            m = jnp.maximum(inter, jnp.max(dmat, axis=1, keepdims=True))
            w = jnp.exp(dmat - m)
            s_inter = jnp.exp(inter - m)
            q = q_ref[:, ks] * scale
            k = k_ref[:, ks]
            qb16 = q.astype(BF16)
            v = v_ref[:, vs].astype(BF16)
            qk = _dot_nt(qb16, k.astype(BF16)) * w
            ct = c_ref[d, h]
            num = s_inter * _dot_nt(qb16, ct.astype(BF16)) + _dot(qk.astype(BF16), v)
            n_row = n_ref[r:r + 1, :]
            den = s_inter * jnp.sum(q * n_row, axis=1, keepdims=True) + jnp.sum(qk, axis=1, keepdims=True)
            o_ref[:, vs] = num / jnp.maximum(jnp.abs(den), jnp.exp(-m))
            g_last = ftot + r_col
            m_new = jnp.maximum(ftot + ms, jnp.max(g_last, axis=0, keepdims=True))
            wk = jnp.exp(g_last - m_new)
            s_dec = jnp.exp(ftot + ms - m_new)
            kw = k * wk
            c_ref[d, h] = s_dec * ct + _dot_tn(v, kw.astype(BF16))
            n_ref[r:r + 1, :] = s_dec * n_row + jnp.sum(kw, axis=0, keepdims=True)
            m_ref[r:r + 1, :] = jnp.broadcast_to(m_new, (1, LANES))

    @pl.when(c == nc - 1)
    def _():
        for d in range(2):
            for h in range(LIN_HEADS):
                cfin_ref[d, h] = c_ref[d, h].T
        nfin_ref[...] = n_ref[...]
        mfin_ref[...] = m_ref[...]


def _mlstm_scan(qkvo, gates, c0, n0, m0, row0, nseq, seqlen):
    nc = seqlen // CHUNK
    r0 = row0 // CHUNK
    hk, hv = LIN_HEADS * LIN_DK, LIN_HEADS * LIN_DV
    qf, qb = _seq_specs(hk, 0, r0, nc)
    kf, kb = _seq_specs(hk, 1, r0, nc)
    vf, vb = _seq_specs(hv, 1, r0, nc)
    gtf, gtb = _seq_specs(LANES, 0, r0, nc)
    c_shape = (2, LIN_HEADS, LIN_DK, LIN_DV)
    c_spec = pl.BlockSpec((None,) + c_shape, lambda b, c: (b, 0, 0, 0, 0))
    v_shape = (2 * LIN_HEADS, LANES)
    v_spec = pl.BlockSpec((None,) + v_shape, lambda b, c: (b, 0, 0))
    o_f = pl.BlockSpec((CHUNK, hv), lambda b, c: (b * nc + c, 0))
    o_b = pl.BlockSpec((CHUNK, hv), lambda b, c: (b * nc + (nc - 1 - c), 0))
    n = nseq * seqlen
    return pl.pallas_call(
        _mlstm_scan_kernel,
        out_shape=(jax.ShapeDtypeStruct((n, hv), F32), jax.ShapeDtypeStruct((n, hv), F32),
                   jax.ShapeDtypeStruct((nseq,) + c_shape, F32),
                   jax.ShapeDtypeStruct((nseq,) + v_shape, F32),
                   jax.ShapeDtypeStruct((nseq,) + v_shape, F32)),
        grid=(nseq, nc),
        in_specs=[qf, kf, vf, qb, kb, vb, gtf, gtb, c_spec, v_spec, v_spec],
        out_specs=(o_f, o_b, c_spec, v_spec, v_spec),
        scratch_shapes=[pltpu.VMEM((2, LIN_HEADS, LIN_DV, LIN_DK), F32),
                        pltpu.VMEM(v_shape, F32), pltpu.VMEM(v_shape, F32)],
        compiler_params=_cparams("parallel", "arbitrary"),
        name="mlstm_scan",
    )(qkvo, qkvo, qkvo, qkvo, qkvo, qkvo, gates, gates, c0, n0, m0)


def _lin_out_kernel(of_ref, ob_ref, gate_ref, gain_ref, w_ref, x_ref, gt_ref, o_ref, *, silu_gate):
    y = _head_rmsnorm(of_ref[...] + ob_ref[...], LIN_DV) * gain_ref[...]
    g = gate_ref[...]
    act = g * _sigmoid(g) if silu_gate else _sigmoid(g)
    o_ref[...] = x_ref[...] + gt_ref[...] * _dot((y * act).astype(BF16), w_ref[...])


def _lin_out(o_f, o_b, proj, gain, w_out, x, mods, layer, group, silu_gate, tm=512):
    t, d = x.shape
    tm = min(tm, group)
    hv = o_f.shape[1]
    gate_block = proj.shape[1] // hv - 1
    return pl.pallas_call(
        functools.partial(_lin_out_kernel, silu_gate=silu_gate),
        out_shape=jax.ShapeDtypeStruct((t, d), F32),
        grid=(t // tm,),
        in_specs=[
            pl.BlockSpec((tm, hv), lambda i: (i, 0)),
            pl.BlockSpec((tm, hv), lambda i: (i, 0)),
            pl.BlockSpec((tm, hv), lambda i: (i, gate_block)),
            pl.BlockSpec((1, hv), lambda i: (0, 0)),
            pl.BlockSpec(w_out.shape, lambda i: (0, 0)),
            pl.BlockSpec((tm, d), lambda i: (i, 0)),
            _mod_spec(layer, 2, tm, group, d),
        ],
        out_specs=pl.BlockSpec((tm, d), lambda i: (i, 0)),
        compiler_params=_cparams("parallel"),
        name="lin_out",
    )(o_f, o_b, proj, gain, w_out, x, mods)


def _mla_in_kernel(x_ref, g_ref, sc_ref, sh_ref, w_ref, gq_ref, gkv_ref, cos_ref, sin_ref,
                   qn_ref, ckv_ref, kpe_ref, kper_ref):
    h = _normmod(x_ref[...], g_ref[...], sc_ref[...], sh_ref[...]).astype(BF16)
    y = _dot(h, w_ref[...])
    ql = y[:, :MLA_Q_LORA]
    qn_ref[...] = (ql * lax.rsqrt(jnp.mean(ql * ql, axis=-1, keepdims=True) + EPS) * gq_ref[...]).astype(BF16)
    o = MLA_Q_LORA
    ckv = y[:, o:o + MLA_KV_LORA]
    ckv_ref[...] = ckv * lax.rsqrt(jnp.mean(ckv * ckv, axis=-1, keepdims=True) + EPS) * gkv_ref[...]
    o += MLA_KV_LORA
    kpe = y[:, o:o + LANES]
    kpe_sw = y[:, o + LANES:o + 2 * LANES]
    kpe_ref[...] = kpe
    kper_ref[...] = kpe * cos_ref[...] + kpe_sw * sin_ref[...]


def _rope_spec(tm, group):
    per = group // tm
    return pl.BlockSpec((None, tm, LANES), lambda i: (jnp.minimum((i * tm) // group, 1), i % per, 0))


def _mla_in(x, norm_g, mods, layer, group, w_cat, g_q, g_kv, cos_t, sin_t, tm=512):
    t, d = x.shape
    tm = min(tm, group)
    row = lambda w: pl.BlockSpec((tm, w), lambda i: (i, 0))
    return pl.pallas_call(
        _mla_in_kernel,
        out_shape=(jax.ShapeDtypeStruct((t, MLA_Q_LORA), BF16), jax.ShapeDtypeStruct((t, MLA_KV_LORA), F32),
                   jax.ShapeDtypeStruct((t, LANES), F32), jax.ShapeDtypeStruct((t, LANES), F32)),
        grid=(t // tm,),
        in_specs=[
            row(d),
            pl.BlockSpec((1, d), lambda i: (0, 0)),
            _mod_spec(layer, 1, tm, group, d),
            _mod_spec(layer, 0, tm, group, d),
            pl.BlockSpec(w_cat.shape, lambda i: (0, 0)),
            pl.BlockSpec((1, MLA_Q_LORA), lambda i: (0, 0)),
            pl.BlockSpec((1, MLA_KV_LORA), lambda i: (0, 0)),
            _rope_spec(tm, group), _rope_spec(tm, group),
        ],
        out_specs=(row(MLA_Q_LORA), row(MLA_KV_LORA), row(LANES), row(LANES)),
        compiler_params=_cparams("parallel"),
        name="mla_in",
    )(x, norm_g.reshape(1, d), mods, mods, w_cat, g_q.reshape(1, -1), g_kv.reshape(1, -1), cos_t, sin_t)


def _mla_q_kernel(qn_ref, w_ref, cos_ref, sin_ref, o_ref):
    y = _dot(qn_ref[...], w_ref[...])
    scale = (MLA_NOPE + MLA_ROPE) ** -0.5
    cos = cos_ref[...] * scale
    sin = sin_ref[...] * scale
    per = MLA_NOPE + 2 * LANES
    for h in range(MLA_HEADS):
        nope = y[:, h * per:h * per + MLA_NOPE]
        pe = y[:, h * per + MLA_NOPE:h * per + MLA_NOPE + LANES]
        pe_sw = y[:, h * per + MLA_NOPE + LANES:(h + 1) * per]
        o_ref[:, h * MLA_QK_PAD:h * MLA_QK_PAD + MLA_NOPE] = (nope * scale).astype(BF16)
        o_ref[:, h * MLA_QK_PAD + MLA_NOPE:(h + 1) * MLA_QK_PAD] = (pe * cos + pe_sw * sin).astype(BF16)


def _mla_q(qn, w_q, cos_t, sin_t, group, tm=256):
    t = qn.shape[0]
    tm = min(tm, group)
    n_out = MLA_HEADS * MLA_QK_PAD
    return pl.pallas_call(
        _mla_q_kernel,
        out_shape=jax.ShapeDtypeStruct((t, n_out), BF16),
        grid=(t // tm,),
        in_specs=[
            pl.BlockSpec((tm, MLA_Q_LORA), lambda i: (i, 0)),
            pl.BlockSpec(w_q.shape, lambda i: (0, 0)),
            _rope_spec(tm, group), _rope_spec(tm, group),
        ],
        out_specs=pl.BlockSpec((tm, n_out), lambda i: (i, 0)),
        compiler_params=_cparams("parallel"),
        name="mla_q",
    )(qn, w_q, cos_t, sin_t)


def _mla_kv_kernel(c_ref, kpe_ref, w_ref, k_ref, v_ref):
    kv = _dot(c_ref[...], w_ref[...])
    kpe = kpe_ref[...]
    per = MLA_NOPE + MLA_V
    for h in range(MLA_HEADS):
        k_ref[h, :, 0:MLA_NOPE] = kv[:, h * per:h * per + MLA_NOPE].astype(BF16)
        k_ref[h, :, MLA_NOPE:MLA_QK_PAD] = kpe
        v_ref[h] = kv[:, h * per + MLA_NOPE:(h + 1) * per].astype(BF16)


def _mla_kv(c_all, kpe_all, w_ukv, tk=256):
    nb, lk, _ = c_all.shape
    tk = next(c for c in (tk, 128, 64) if lk % c == 0)
    return pl.pallas_call(
        _mla_kv_kernel,
        out_shape=(jax.ShapeDtypeStruct((nb, MLA_HEADS, lk, MLA_QK_PAD), BF16),
                   jax.ShapeDtypeStruct((nb, MLA_HEADS, lk, MLA_V), BF16)),
        grid=(nb, lk // tk),
        in_specs=[
            pl.BlockSpec((None, tk, MLA_KV_LORA), lambda b, j: (b, j, 0)),
            pl.BlockSpec((None, tk, LANES), lambda b, j: (b, j, 0)),
            pl.BlockSpec(w_ukv.shape, lambda b, j: (0, 0)),
        ],
        out_specs=(pl.BlockSpec((None, MLA_HEADS, tk, MLA_QK_PAD), lambda b, j: (b, 0, j, 0)),
                   pl.BlockSpec((None, MLA_HEADS, tk, MLA_V), lambda b, j: (b, 0, j, 0))),
        compiler_params=_cparams("parallel", "parallel"),
        name="mla_kv",
    )(c_all, kpe_all, w_ukv)


def _attn_kernel(q_ref, k_ref, v_ref, o_ref):
    s = _dot_nt(q_ref[...], k_ref[...])
    p = jnp.exp(s - jnp.max(s, axis=-1, keepdims=True))
    l = jnp.sum(p, axis=-1, keepdims=True)
    o_ref[...] = (_dot(p.astype(BF16), v_ref[...]) / l).astype(BF16)


def _attention(q, k, v, row0, seqlen, tq=256):
    nb, nh, lk, _ = k.shape
    tq = min(tq, seqlen)
    nq = seqlen // tq
    r0 = row0 // tq
    return pl.pallas_call(
        _attn_kernel,
        out_shape=jax.ShapeDtypeStruct((nb * seqlen, nh * MLA_V), BF16),
        grid=(nb, nh, nq),
        in_specs=[
            pl.BlockSpec((tq, MLA_QK_PAD), lambda b, h, i: (r0 + b * nq + i, h)),
            pl.BlockSpec((None, None, lk, MLA_QK_PAD), lambda b, h, i: (b, h, 0, 0)),
            pl.BlockSpec((None, None, lk, MLA_V), lambda b, h, i: (b, h, 0, 0)),
        ],
        out_specs=pl.BlockSpec((tq, MLA_V), lambda b, h, i: (b * nq + i, h)),
        compiler_params=_cparams("parallel", "parallel", "arbitrary"),
        name="mla_attention",
    )(q, k, v)


def _proj_res_kernel(a_ref, w_ref, x_ref, gt_ref, o_ref):
    o_ref[...] = x_ref[...] + gt_ref[...] * _dot(a_ref[...], w_ref[...])


def _proj_res(a, w, x, mods, layer, group, tm=512):
    t, d = x.shape
    tm = min(tm, group)
    return pl.pallas_call(
        _proj_res_kernel,
        out_shape=jax.ShapeDtypeStruct((t, d), F32),
        grid=(t // tm,),
        in_specs=[
            pl.BlockSpec((tm, a.shape[1]), lambda i: (i, 0)),
            pl.BlockSpec(w.shape, lambda i: (0, 0)),
            pl.BlockSpec((tm, d), lambda i: (i, 0)),
            _mod_spec(layer, 2, tm, group, d),
        ],
        out_specs=pl.BlockSpec((tm, d), lambda i: (i, 0)),
        compiler_params=_cparams("parallel"),
        name="proj_res",
    )(a, w, x, mods)


def _ffn_kernel(x_ref, g_ref, sc_ref, sh_ref, gt_ref, wg_ref, wu_ref, wd_ref, o_ref, h_scr, acc_scr):
    j = pl.program_id(1)

    @pl.when(j == 0)
    def _():
        h_scr[...] = _normmod(x_ref[...], g_ref[...], sc_ref[...], sh_ref[...]).astype(BF16)
        acc_scr[...] = jnp.zeros_like(acc_scr)

    h = h_scr[...]
    a = _dot(h, wg_ref[...])
    z = (a * _sigmoid(a) * _dot(h, wu_ref[...])).astype(BF16)
    acc_scr[...] += _dot(z, wd_ref[...])

    @pl.when(j == pl.num_programs(1) - 1)
    def _():
        o_ref[...] = x_ref[...] + gt_ref[...] * acc_scr[...]


def _ffn(x, norm_g, mods, layer, group, w_gate, w_up, w_down, tm=1024, tf=256):
    t, d = x.shape
    f = w_gate.shape[1]
    tm = min(tm, group)
    return pl.pallas_call(
        _ffn_kernel,
        out_shape=jax.ShapeDtypeStruct((t, d), F32),
        grid=(t // tm, f // tf),
        in_specs=[
            pl.BlockSpec((tm, d), lambda i, j: (i, 0)),
            pl.BlockSpec((1, d), lambda i, j: (0, 0)),
            _mod_spec(layer, 4, tm, group, d),
            _mod_spec(layer, 3, tm, group, d),
            _mod_spec(layer, 5, tm, group, d),
            pl.BlockSpec((d, tf), lambda i, j: (0, j)),
            pl.BlockSpec((d, tf), lambda i, j: (0, j)),
            pl.BlockSpec((tf, d), lambda i, j: (j, 0)),
        ],
        out_specs=pl.BlockSpec((tm, d), lambda i, j: (i, 0)),
        scratch_shapes=[pltpu.VMEM((tm, d), BF16), pltpu.VMEM((tm, d), F32)],
        compiler_params=_cparams("parallel", "arbitrary"),
        name="ffn",
    )(x, norm_g.reshape(1, d), mods, mods, mods, w_gate, w_up, w_down)


def _router_kernel(x_ref, g_ref, sc_ref, sh_ref, w_ref, h_ref, e_ref, p_ref):
    h = _normmod(x_ref[...], g_ref[...], sc_ref[...], sh_ref[...])
    h_ref[...] = h
    logits = _dot3(h, w_ref[...])
    lane = lax.broadcasted_iota(jnp.int32, logits.shape, 1)
    lane_f = lane.astype(F32)
    logits = jnp.where(lane < N_EXPERTS, logits, -jnp.inf)
    l1 = jnp.max(logits, axis=-1, keepdims=True)
    e1 = jnp.min(jnp.where(logits == l1, lane_f, float(LANES)), axis=-1, keepdims=True)
    rest = jnp.where(lane_f == e1, -jnp.inf, logits)
    l2 = jnp.max(rest, axis=-1, keepdims=True)
    e2 = jnp.min(jnp.where(rest == l2, lane_f, float(LANES)), axis=-1, keepdims=True)
    ex = jnp.exp(l2 - l1)
    p1 = 1.0 / (1.0 + ex)
    p2 = ex / (1.0 + ex)
    e_ref[...] = jnp.where(lane == 0, e1, jnp.where(lane == 1, e2, 0.0)).astype(jnp.int32)
    p_ref[...] = jnp.where(lane == 0, p1, jnp.where(lane == 1, p2, 0.0))


def _router(x, norm_g, mods, layer, group, w_router, tm=512):
    t, d = x.shape
    tm = min(tm, group)
    return pl.pallas_call(
        _router_kernel,
        out_shape=(jax.ShapeDtypeStruct((t, d), F32), jax.ShapeDtypeStruct((t, LANES), jnp.int32),
                   jax.ShapeDtypeStruct((t, LANES), F32)),
        grid=(t // tm,),
        in_specs=[
            pl.BlockSpec((tm, d), lambda i: (i, 0)),
            pl.BlockSpec((1, d), lambda i: (0, 0)),
            _mod_spec(layer, 4, tm, group, d),
            _mod_spec(layer, 3, tm, group, d),
            pl.BlockSpec(w_router.shape, lambda i: (0, 0)),
        ],
        out_specs=(pl.BlockSpec((tm, d), lambda i: (i, 0)), pl.BlockSpec((tm, LANES), lambda i: (i, 0)),
                   pl.BlockSpec((tm, LANES), lambda i: (i, 0))),
        compiler_params=_cparams("parallel"),
        name="moe_router",
    )(x, norm_g.reshape(1, d), mods, mods, w_router)


def _row_copy(src_hbm, row, dst_vmem, r, sem):
    return pltpu.make_async_copy(src_hbm.at[pl.ds(row, 1)], dst_vmem.at[pl.ds(r, 1)], sem)


def _gather_kernel(idx_ref, h_hbm, o_ref, sem):
    t = pl.program_id(0)
    rows = o_ref.shape[0]

    def start(r, carry):
        _row_copy(h_hbm, idx_ref[t, r], o_ref, r, sem).start()
        return carry

    def wait(r, carry):
        _row_copy(h_hbm, 0, o_ref, r, sem).wait()
        return carry

    lax.fori_loop(0, rows, start, 0, unroll=8)
    lax.fori_loop(0, rows, wait, 0, unroll=8)


def _moe_gather(slot_tok, h):
    n_tiles, tile = slot_tok.shape
    d = h.shape[1]
    return pl.pallas_call(
        _gather_kernel,
        out_shape=jax.ShapeDtypeStruct((n_tiles * tile, d), F32),
        grid_spec=pltpu.PrefetchScalarGridSpec(
            num_scalar_prefetch=1,
            grid=(n_tiles,),
            in_specs=[pl.BlockSpec(memory_space=pl.ANY)],
            out_specs=pl.BlockSpec((tile, d), lambda t, idx: (t, 0)),
            scratch_shapes=[pltpu.SemaphoreType.DMA],
        ),
        compiler_params=_cparams("arbitrary"),
        name="moe_gather",
    )(slot_tok, h)


def _expert_kernel(te_ref, x_ref, wg_ref, wu_ref, wd_ref, o_ref):
    j = pl.program_id(1)
    h = x_ref[...].astype(BF16)
    a = _dot(h, wg_ref[...])
    z = (a * _sigmoid(a) * _dot(h, wu_ref[...])).astype(BF16)
    y = _dot(z, wd_ref[...])

    @pl.when(j == 0)
    def _():
        o_ref[...] = y

    @pl.when(j > 0)
    def _():
        o_ref[...] += y


def _moe_experts(tile_e, xb, w_gate, w_up, w_down, tf=512):
    n_tiles = tile_e.shape[0]
    _, d, f = w_gate.shape
    return pl.pallas_call(
        _expert_kernel,
        out_shape=jax.ShapeDtypeStruct(xb.shape, F32),
        grid_spec=pltpu.PrefetchScalarGridSpec(
            num_scalar_prefetch=1,
            grid=(n_tiles, f // tf),
            in_specs=[
                pl.BlockSpec((MOE_TILE, d), lambda t, j, te: (t, 0)),
                pl.BlockSpec((None, d, tf), lambda t, j, te: (te[t], 0, j)),
                pl.BlockSpec((None, d, tf), lambda t, j, te: (te[t], 0, j)),
                pl.BlockSpec((None, tf, d), lambda t, j, te: (te[t], j, 0)),
            ],
            out_specs=pl.BlockSpec((MOE_TILE, d), lambda t, j, te: (t, 0)),
        ),
        compiler_params=_cparams("parallel", "arbitrary"),
        name="moe_experts",
    )(tile_e, xb, w_gate, w_up, w_down)


def _combine_kernel(dst_ref, yb_hbm, x_ref, p_ref, gt_ref, o_ref, buf0, buf1, sem):
    i = pl.program_id(0)
    rows = x_ref.shape[0]

    def start(r, carry):
        _row_copy(yb_hbm, dst_ref[i, 2 * r], buf0, r, sem).start()
        _row_copy(yb_hbm, dst_ref[i, 2 * r + 1], buf1, r, sem).start()
        return carry

    def wait(r, carry):
        _row_copy(yb_hbm, 0, buf0, r, sem).wait()
        _row_copy(yb_hbm, 0, buf1, r, sem).wait()
        return carry

    lax.fori_loop(0, rows, start, 0, unroll=8)
    lax.fori_loop(0, rows, wait, 0, unroll=8)
    p = p_ref[...]
    y = buf0[...] * p[:, 0:1] + buf1[...] * p[:, 1:2]
    o_ref[...] = x_ref[...] + gt_ref[...] * y


def _moe_combine(dest, yb, x, probs, mods, layer, group, tm=256):
    t, d = x.shape
    tm = min(tm, group)
    return pl.pallas_call(
        _combine_kernel,
        out_shape=jax.ShapeDtypeStruct((t, d), F32),
        grid_spec=pltpu.PrefetchScalarGridSpec(
            num_scalar_prefetch=1,
            grid=(t // tm,),
            in_specs=[
                pl.BlockSpec(memory_space=pl.ANY),
                pl.BlockSpec((tm, d), lambda i, dst: (i, 0)),
                pl.BlockSpec((tm, LANES), lambda i, dst: (i, 0)),
                _mod_spec(layer, 5, tm, group, d),
            ],
            out_specs=pl.BlockSpec((tm, d), lambda i, dst: (i, 0)),
            scratch_shapes=[pltpu.VMEM((tm, d), F32), pltpu.VMEM((tm, d), F32), pltpu.SemaphoreType.DMA],
        ),
        compiler_params=_cparams("arbitrary"),
        name="moe_combine",
    )(dest.reshape(t // tm, 2 * tm), yb, x, probs, mods)


def _moe_layer(x, norm_g, mods, layer, group, w_router, w_gate, w_up, w_down):
    t, d = x.shape
    h, e_slab, p_slab = _router(x, norm_g, mods, layer, group, w_router)
    flat_e = e_slab[:, :2].reshape(-1)
    onehot = (flat_e[:, None] == jnp.arange(N_EXPERTS, dtype=jnp.int32)[None, :]).astype(jnp.int32)
    csum = jnp.cumsum(onehot, axis=0)
    rank = jnp.take_along_axis(csum, flat_e[:, None], axis=1)[:, 0] - 1
    counts = csum[-1]
    padded = (counts + MOE_TILE - 1) // MOE_TILE * MOE_TILE
    pad_end = jnp.cumsum(padded)
    dest = (pad_end - padded)[flat_e] + rank
    n_tiles = (2 * t) // MOE_TILE + N_EXPERTS
    flat_tok = jnp.repeat(jnp.arange(t, dtype=jnp.int32), 2)
    slot_tok = jnp.zeros((n_tiles * MOE_TILE,), jnp.int32).at[dest].set(flat_tok)
    tile_start = jnp.arange(n_tiles, dtype=jnp.int32) * MOE_TILE
    tile_e = jnp.minimum(jnp.searchsorted(pad_end, tile_start, side="right"), N_EXPERTS - 1).astype(jnp.int32)
    xb = _moe_gather(slot_tok.reshape(n_tiles, MOE_TILE), h)
    yb = _moe_experts(tile_e, xb, w_gate, w_up, w_down)
    return _moe_combine(dest.astype(jnp.int32), yb, x, p_slab, mods, layer, group)


def _final_norm_kernel(x_ref, g_ref, o_ref):
    x = x_ref[...]
    o_ref[...] = x * lax.rsqrt(jnp.mean(x * x, axis=-1, keepdims=True) + EPS) * g_ref[...]


def _final_norm(x, g, tm=256):
    t, d = x.shape
    return pl.pallas_call(
        _final_norm_kernel,
        out_shape=jax.ShapeDtypeStruct((t, d), F32),
        grid=(t // tm,),
        in_specs=[pl.BlockSpec((tm, d), lambda i: (i, 0)), pl.BlockSpec((1, d), lambda i: (0, 0))],
        out_specs=pl.BlockSpec((tm, d), lambda i: (i, 0)),
        compiler_params=_cparams("parallel"),
        name="final_norm",
    )(x, g.reshape(1, d))


def _pad_cols(w, n):
    return jnp.pad(w, ((0, 0), (0, n - w.shape[1])))


def _rope_swap(w):
    half = MLA_ROPE // 4
    perm = jnp.concatenate([jnp.arange(half, 2 * half), jnp.arange(0, half),
                            jnp.arange(3 * half, 4 * half), jnp.arange(2 * half, 3 * half)])
    return w[:, perm]


def _rope_tables(seqlen):
    pos = jnp.arange(seqlen, dtype=F32)
    r = jnp.floor(pos / GRID_W)
    col = pos - r * GRID_W
    axis = MLA_ROPE // 2
    inv = ROPE_THETA ** (-jnp.arange(0, axis, 2, dtype=F32) / axis)
    ar, ac = r[:, None] * inv, col[:, None] * inv
    zeros = jnp.zeros((seqlen, LANES - MLA_ROPE), F32)
    cos = jnp.concatenate([jnp.cos(ar), jnp.cos(ar), jnp.cos(ac), jnp.cos(ac), zeros + 1.0], axis=1)
    sin = jnp.concatenate([-jnp.sin(ar), jnp.sin(ar), -jnp.sin(ac), jnp.sin(ac), zeros], axis=1)
    ident_c = jnp.ones((seqlen, LANES), F32)
    ident_s = jnp.zeros((seqlen, LANES), F32)
    return jnp.stack([ident_c, cos]), jnp.stack([ident_s, sin])


def kernel(x_prompt, x_sample, state_gla, state_mlstm_c, state_mlstm_n, state_mlstm_m, cache_mla_ckv, cache_mla_kpe, c, c_ctx, ada_w, ada_b, norm1_g, norm2_g, final_g, gla_w_in, gla_w_gk1, gla_w_gk2, gla_b_gk, gla_g_out, gla_w_out, mlstm_w_in, mlstm_w_gates, mlstm_b_gates, mlstm_g_out, mlstm_w_out, mla_w_dq, mla_g_q, mla_w_uq, mla_w_dkv, mla_g_kv, mla_w_ukv, mla_w_out, ffn_w_gate, ffn_w_up, ffn_w_down, moe_w_router, moe_w_gate, moe_w_up, moe_w_down):
    bp, sp, d = x_prompt.shape
    bs, ss, _ = x_sample.shape
    depth = ada_w.shape[0]
    tp = bp * sp
    group = ss
    assert tp == group, "prompt tokens must fill exactly one conditioning group"
    t = tp + bs * ss

    x = jnp.concatenate([x_prompt.reshape(tp, d), x_sample.reshape(bs * ss, d)], axis=0)
    cond = jnp.concatenate([c_ctx[None, :], c], axis=0)
    cond = jnp.pad(cond, ((0, N_MODS_PAD - cond.shape[0]), (0, 0)))
    mods = _ada_mods(cond, ada_w, ada_b)
    cos_t, sin_t = _rope_tables(ss)

    new_gla, new_mc, new_mn, new_mm, new_ckv, new_kpe = [], [], [], [], [], []
    hk = LIN_HEADS * LIN_DK
    for i in range(depth):
        j = i // 3
        if i % 3 == 0:
            w1 = _pad_cols(jnp.concatenate([gla_w_gk1[j, 0], gla_w_gk1[j, 1]], axis=1), LANES).astype(BF16)
            rank = gla_w_gk2.shape[2]
            w2 = jnp.zeros((LANES, 2 * hk), F32)
            w2 = w2.at[:rank, :hk].set(gla_w_gk2[j, 0]).at[rank:2 * rank, hk:].set(gla_w_gk2[j, 1]).astype(BF16)
            b_gk = gla_b_gk[j].reshape(1, 2 * hk)
            qkvg, lg = _mixer_in(_gla_in_kernel, x, norm1_g[i], mods, i, group, gla_w_in[j].astype(BF16),
                                 (w1, w2, b_gk), 2 * hk)
            zero_state = jnp.zeros((bp, 2, LIN_HEADS, LIN_DK, LIN_DV), F32)
            of_p, ob_p, s_fin = _gla_scan(qkvg, lg, zero_state, 0, bp, sp)
            of_s, ob_s, _ = _gla_scan(qkvg, lg, state_gla[:, j], tp, bs, ss)
            new_gla.append(s_fin)
            gain = jnp.tile(gla_g_out[j], LIN_HEADS).reshape(1, -1)
            x = _lin_out(jnp.concatenate([of_p, of_s]), jnp.concatenate([ob_p, ob_s]), qkvg, gain,
                         gla_w_out[j].astype(BF16), x, mods, i, group, True)
        elif i % 3 == 1:
            wg = _pad_cols(jnp.concatenate([mlstm_w_gates[j, 0], mlstm_w_gates[j, 1]], axis=1), LANES)
            bg = _pad_cols(mlstm_b_gates[j].reshape(1, -1), LANES)
            qkvo, gates = _mixer_in(_mlstm_in_kernel, x, norm1_g[i], mods, i, group, mlstm_w_in[j].astype(BF16),
                                    (wg, bg), LANES)
            nh2 = 2 * LIN_HEADS
            c0 = jnp.zeros((bp, 2, LIN_HEADS, LIN_DK, LIN_DV), F32)
            v0 = jnp.zeros((bp, nh2, LANES), F32)
            of_p, ob_p, c_fin, n_fin, m_fin = _mlstm_scan(qkvo, gates, c0, v0, v0, 0, bp, sp)
            n0 = state_mlstm_n[:, j].reshape(bs, nh2, LIN_DK)
            m0 = jnp.broadcast_to(state_mlstm_m[:, j].reshape(bs, nh2, 1), (bs, nh2, LANES))
            of_s, ob_s, _, _, _ = _mlstm_scan(qkvo, gates, state_mlstm_c[:, j], n0, m0, tp, bs, ss)
            new_mc.append(c_fin)
            new_mn.append(n_fin.reshape(bp, 2, LIN_HEADS, LIN_DK))
            new_mm.append(m_fin[:, :, 0].reshape(bp, 2, LIN_HEADS))
            x = _lin_out(jnp.concatenate([of_p, of_s]), jnp.concatenate([ob_p, ob_s]), qkvo,
                         mlstm_g_out[j].reshape(1, -1), mlstm_w_out[j].astype(BF16), x, mods, i, group, False)
        else:
            w_kpe = mla_w_dkv[j][:, MLA_KV_LORA:]
            w_cat = jnp.concatenate([mla_w_dq[j], mla_w_dkv[j][:, :MLA_KV_LORA], _pad_cols(w_kpe, LANES),
                                     _pad_cols(_rope_swap(w_kpe), LANES)], axis=1).astype(BF16)
            qn, ckv, kpe, kpe_rot = _mla_in(x, norm1_g[i], mods, i, group, w_cat, mla_g_q[j], mla_g_kv[j],
                                            cos_t, sin_t)
            wq = mla_w_uq[j].reshape(MLA_Q_LORA, MLA_HEADS, MLA_NOPE + MLA_ROPE)
            wq_pe = wq[:, :, MLA_NOPE:].reshape(MLA_Q_LORA * MLA_HEADS, MLA_ROPE)
            pad3 = lambda a: _pad_cols(a, LANES).reshape(MLA_Q_LORA, MLA_HEADS, LANES)
            w_q = jnp.concatenate([wq[:, :, :MLA_NOPE], pad3(wq_pe), pad3(_rope_swap(wq_pe))], axis=2)
            q = _mla_q(qn, w_q.reshape(MLA_Q_LORA, -1).astype(BF16), cos_t, sin_t, group)
            w_ukv = mla_w_ukv[j].astype(BF16)
            new_ckv.append(ckv[:tp].reshape(bp, sp, MLA_KV_LORA))
            new_kpe.append(kpe[:tp, :MLA_ROPE].reshape(bp, sp, MLA_ROPE))
            k_p, v_p = _mla_kv(ckv[:tp].reshape(bp, sp, -1).astype(BF16),
                               kpe[:tp].reshape(bp, sp, LANES).astype(BF16), w_ukv)
            o_p = _attention(q, k_p, v_p, 0, sp)
            c_all = jnp.concatenate([ckv[tp:].reshape(bs, ss, -1), cache_mla_ckv[:, j]], axis=1).astype(BF16)
            ctx_kpe = jnp.pad(cache_mla_kpe[:, j], ((0, 0), (0, 0), (0, LANES - MLA_ROPE)))
            kpe_all = jnp.concatenate([kpe_rot[tp:].reshape(bs, ss, LANES), ctx_kpe], axis=1).astype(BF16)
            k_s, v_s = _mla_kv(c_all, kpe_all, w_ukv)
            o_s = _attention(q, k_s, v_s, tp, ss)
            x = _proj_res(jnp.concatenate([o_p, o_s]), mla_w_out[j].astype(BF16), x, mods, i, group)
        f = i // 2
        if i % 2 == 0:
            x = _ffn(x, norm2_g[i], mods, i, group, ffn_w_gate[f].astype(BF16), ffn_w_up[f].astype(BF16),
                     ffn_w_down[f].astype(BF16))
        else:
            x = _moe_layer(x, norm2_g[i], mods, i, group, _pad_cols(moe_w_router[f], LANES),
                           moe_w_gate[f].astype(BF16), moe_w_up[f].astype(BF16), moe_w_down[f].astype(BF16))

    y = _final_norm(x, final_g)
    return (y[:tp].reshape(bp, sp, d), y[tp:].reshape(bs, ss, d),
            jnp.stack(new_gla, axis=1), jnp.stack(new_mc, axis=1), jnp.stack(new_mn, axis=1),
            jnp.stack(new_mm, axis=1), jnp.stack(new_ckv, axis=1), jnp.stack(new_kpe, axis=1))
```
